```python
import math
import jax, jax.numpy as jnp
from jax import lax
import numpy as np

D_MODEL = 2048
BATCH = 4
SEQ = 4096
DEPTH = 2
DEC_BATCH = 32
DEC_SEQ = 64
PAST_LEN = 1024

CHUNK = 64
Q_BLOCK = 128
D_Q_BLOCK = 64
EPS = 1e-6
NEG_INF = -1e30
H_A = 8
DH_A = 64
A_BAND_CHUNKS = 8
A_WINDOW = A_BAND_CHUNKS * CHUNK
A_CLIP = 256
H_B = 4
DH_B = 64
H_C = 4
C_NOPE = 64
C_ROPE = 32
C_V = 128
C_Q_RANK = 384
C_KV_RANK = 256
ROPE_THETA = 10000.0
H_D = 8
DH_D = 64
IDX_HEADS = 8
IDX_DIM = 64
IDX_TOPK = 256
T5_BUCKETS = 32
T5_MAX_DIST = 128
N_BRANCH = 4
BRANCH_W = 512
D_FF = 5632
N_EXPERTS = 8
TOP_K = 2
D_FF_E = 7168
IN_SIZES = ((H_A * DH_A,) * 3 + (H_B * 2 * DH_B,) * 3 + (C_Q_RANK, C_KV_RANK, C_ROPE)
            + (H_D * DH_D,) * 3 + (IDX_HEADS * IDX_DIM, IDX_DIM, IDX_HEADS))
IN_COLS = sum(IN_SIZES)

kernel_name = 'chunk_causal_hybrid_encoder_step'


def rmsnorm(x, g):
    xf = x.astype(jnp.float32)
    y = xf * lax.rsqrt(jnp.mean(xf * xf, axis=-1, keepdims=True) + EPS)
    return (y * g.astype(jnp.float32)).astype(x.dtype)


def masked_softmax(logits, mask):
    return jax.nn.softmax(jnp.where(mask, logits.astype(jnp.float32), NEG_INF), axis=-1)


def chunk_causal(qpos, kpos):
    return (kpos[None, :] // CHUNK) <= (qpos[:, None] // CHUNK)


def t5_bucket(rel):
    nb = T5_BUCKETS // 2
    max_exact = nb // 2
    ret = jnp.where(rel > 0, nb, 0)
    n = jnp.abs(rel)
    nf = jnp.maximum(n, 1).astype(jnp.float32)
    large = max_exact + (jnp.log(nf / max_exact) / math.log(T5_MAX_DIST / max_exact)
                         * (nb - max_exact)).astype(jnp.int32)
    large = jnp.minimum(large, nb - 1)
    return ret + jnp.where(n < max_exact, n, large)


def t5_rel_bias(qpos, kpos, tab):
    return jnp.transpose(tab[t5_bucket(kpos[None, :] - qpos[:, None])], (2, 0, 1))


def apply_rope(x, pos):
    half = x.shape[-1] // 2
    freqs = ROPE_THETA ** (-jnp.arange(half, dtype=jnp.float32) / half)
    ang = pos.astype(jnp.float32)[:, None] * freqs
    shape = (pos.shape[0],) + (1,) * (x.ndim - 3) + (half,)
    cos = jnp.cos(ang).reshape(shape)
    sin = jnp.sin(ang).reshape(shape)
    xf = x.astype(jnp.float32)
    x1, x2 = xf[..., :half], xf[..., half:]
    return jnp.concatenate([x1 * cos - x2 * sin, x2 * cos + x1 * sin], axis=-1).astype(x.dtype)


def over_query_blocks(fn, qs, qpos, block):
    T = qpos.shape[0]
    if T <= block:
        return fn(qs, qpos)
    nb = T // block
    qs_b = tuple(jnp.moveaxis(q.reshape(q.shape[0], nb, block, *q.shape[2:]), 1, 0) for q in qs)
    out = lax.map(lambda a: fn(a[0], a[1]), (qs_b, qpos.reshape(nb, block)))
    out = jnp.moveaxis(out, 0, 1)
    return out.reshape(out.shape[0], T, *out.shape[3:])


def band_attention(q, k, v, qpos, kpos, rel_table):
    qc = (qpos // CHUNK)[:, :, None]
    kc = (kpos // CHUNK)[:, None, :]
    mask = (kc <= qc) & (kc >= qc - A_BAND_CHUNKS) & (kpos[:, None, :] >= 0)
    rel = jnp.clip(kpos[:, None, :] - qpos[:, :, None], -A_CLIP, A_CLIP) + A_CLIP
    bias = jnp.moveaxis(rel_table[rel], -1, 1)
    logits = jnp.einsum('bnqhd,bnkhd->bnhqk', q, k) * DH_A ** -0.5 + bias
    p = masked_softmax(logits, mask[:, None])
    return jnp.einsum('bnhqk,bnkhd->bnqhd', p.astype(v.dtype), v)


def diff_core(q, qpos, k, v, kpos, tab, lam):
    bias = t5_rel_bias(qpos, kpos, tab)
    logits = jnp.einsum('bqhjd,bkhjd->bjhqk', q, k) * DH_B ** -0.5 + bias
    p = masked_softmax(logits, chunk_causal(qpos, kpos))
    attn = p[:, 0] - lam * p[:, 1]
    return jnp.einsum('bhqk,bkhe->bqhe', attn.astype(v.dtype), v)


def mla_core(q_lat, q_rope, qpos, lat, kr, kpos):
    logits = (jnp.einsum('bqhr,bkr->bhqk', q_lat, lat)
              + jnp.einsum('bqhe,bke->bhqk', q_rope, kr)) * (C_NOPE + C_ROPE) ** -0.5
    p = masked_softmax(logits, chunk_causal(qpos, kpos))
    return jnp.einsum('bhqk,bkr->bqhr', p.astype(lat.dtype), lat)


def dsa_core(q, qi, wi, qpos, k, v, kidx, kpos, tab, topk):
    act = jax.nn.relu(jnp.einsum('bqhd,bkd->bqhk', qi, kidx) * IDX_DIM ** -0.5)
    score = jnp.einsum('bqh,bqhk->bqk', wi, act).astype(jnp.float32)
    score = jnp.where(chunk_causal(qpos, kpos), score, NEG_INF)
    _, sel = lax.top_k(score, topk)
    k_sel = jax.vmap(lambda kk, ii: kk[ii])(k, sel)
    v_sel = jax.vmap(lambda vv, ii: vv[ii])(v, sel)
    kpos_sel = kpos[sel]
    valid = (kpos_sel // CHUNK) <= (qpos // CHUNK)[None, :, None]
    bias = jnp.moveaxis(tab[t5_bucket(kpos_sel - qpos[None, :, None])], -1, 2)
    logits = jnp.einsum('bqhd,bqkhd->bqhk', q, k_sel) * DH_D ** -0.5 + bias
    p = masked_softmax(logits, valid[:, :, None, :])
    return jnp.einsum('bqhk,bqkhd->bqhd', p.astype(v.dtype), v_sel)


def token_mixers(h, l, past, w_in, a_rel_bias, t5_bias, b_lambda, b_subln,
                 c_q_norm, c_kv_norm, c_w_uq, c_w_uk, c_w_uv):
    bsz, T, _ = h.shape
    splits = np.cumsum(IN_SIZES)[:-1].tolist()
    (a_q, a_k, a_v, b_q, b_k, b_v, c_cq, c_ckv, c_kr,
     d_q, d_k, d_v, d_qi, d_ki, d_w) = jnp.split(h @ w_in[l], splits, axis=-1)
    past_len = 0 if past is None else past[1].shape[1]
    qpos = past_len + jnp.arange(T)
    kpos = jnp.arange(past_len + T)

    aq = a_q.reshape(bsz, T, H_A, DH_A)
    a_new = jnp.stack([a_k.reshape(bsz, T, H_A, DH_A), a_v.reshape(bsz, T, H_A, DH_A)], axis=2)
    if past is None:
        nc = T // CHUNK
        kvc = a_new.reshape(bsz, nc, CHUNK, 2, H_A, DH_A)
        kvc = jnp.pad(kvc, ((0, 0), (A_BAND_CHUNKS, 0), (0, 0), (0, 0), (0, 0), (0, 0)))
        band = jnp.concatenate([kvc[:, j:j + nc] for j in range(A_BAND_CHUNKS + 1)], axis=2)
        cid = jnp.arange(nc)[:, None] + jnp.arange(-A_BAND_CHUNKS, 1)[None, :]
        band_pos = (cid[:, :, None] * CHUNK + jnp.arange(CHUNK)).reshape(nc, -1)
        a_o = band_attention(aq.reshape(bsz, nc, CHUNK, H_A, DH_A), band[:, :, :, 0], band[:, :, :, 1],
                             qpos.reshape(nc, CHUNK), band_pos, a_rel_bias[l])
        a_state = a_new[:, T - min(A_WINDOW, T):]
    else:
        a_len = past[0].shape[1]
        kva = jnp.concatenate([past[0], a_new], axis=1)
        a_o = band_attention(aq[:, None], kva[:, None, :, 0], kva[:, None, :, 1], qpos[None],
                             jnp.arange(past_len - a_len, past_len + T)[None], a_rel_bias[l])
        a_state = a_new
    a_out = a_o.reshape(bsz, T, H_A * DH_A)

    bq = b_q.reshape(bsz, T, H_B, 2, DH_B)
    b_new = jnp.stack([b_k.reshape(bsz, T, H_B, 2 * DH_B), b_v.reshape(bsz, T, H_B, 2 * DH_B)], axis=2)
    kvb = b_new if past is None else jnp.concatenate([past[1], b_new], axis=1)
    bk = kvb[:, :, 0].reshape(bsz, past_len + T, H_B, 2, DH_B)
    bv = kvb[:, :, 1]
    lam_init = 0.8 - 0.6 * math.exp(-0.3 * l)
    lq1, lk1, lq2, lk2 = b_lambda[l].astype(jnp.float32)
    lam = jnp.exp(jnp.sum(lq1 * lk1)) - jnp.exp(jnp.sum(lq2 * lk2)) + lam_init
    tab_b = t5_bias[:, :H_B]
    b_o = over_query_blocks(lambda qs, qp: diff_core(qs[0], qp, bk, bv, kpos, tab_b, lam),
                            (bq,), qpos, Q_BLOCK)
    b_out = (rmsnorm(b_o, b_subln[l]) * (1.0 - lam_init)).reshape(bsz, T, H_B * 2 * DH_B)

    cqh = jnp.einsum('btr,rhe->bthe', rmsnorm(c_cq, c_q_norm[l]), c_w_uq[l])
    q_nope = cqh[..., :C_NOPE]
    q_rope = apply_rope(cqh[..., C_NOPE:], qpos)
    c_lat_new = rmsnorm(c_ckv, c_kv_norm[l])
    c_kr_new = apply_rope(c_kr, qpos)
    lat = c_lat_new if past is None else jnp.concatenate([past[2], c_lat_new], axis=1)
    kr = c_kr_new if past is None else jnp.concatenate([past[3], c_kr_new], axis=1)
    q_lat = jnp.einsum('bthn,rhn->bthr', q_nope, c_w_uk[l])
    o_lat = over_query_blocks(lambda qs, qp: mla_core(qs[0], qs[1], qp, lat, kr, kpos),
                              (q_lat, q_rope), qpos, Q_BLOCK)
    c_out = jnp.einsum('bthr,rhe->bthe', o_lat, c_w_uv[l]).reshape(bsz, T, H_C * C_V)

    dq = d_q.reshape(bsz, T, H_D, DH_D)
    d_new = jnp.stack([d_k.reshape(bsz, T, H_D, DH_D), d_v.reshape(bsz, T, H_D, DH_D)], axis=2)
    dqi = d_qi.reshape(bsz, T, IDX_HEADS, IDX_DIM)
    dw = d_w * IDX_HEADS ** -0.5
    kvd = d_new if past is None else jnp.concatenate([past[4], d_new], axis=1)
    kidx = d_ki if past is None else jnp.concatenate([past[5], d_ki], axis=1)
    dk_all = kvd[:, :, 0]
    dv_all = kvd[:, :, 1]
    topk = min(IDX_TOPK, (past_len + T) // 4)
    tab_d = t5_bias[:, H_B:]
    d_o = over_query_blocks(
        lambda qs, qp: dsa_core(qs[0], qs[1], qs[2], qp, dk_all, dv_all, kidx, kpos, tab_d, topk),
        (dq, dqi, dw), qpos, D_Q_BLOCK)
    d_out = d_o.reshape(bsz, T, H_D * DH_D)

    return (a_out, b_out, c_out, d_out), (a_state, b_new, c_lat_new, c_kr_new, d_new, d_ki)


def swiglu(h, w1, w2):
    g, u = jnp.split(h @ w1, 2, axis=-1)
    return (jax.nn.silu(g) * u) @ w2


def moe_swiglu(h, router, w1, w2):
    logits = (h @ router).astype(jnp.float32)
    top_v, top_i = lax.top_k(logits, TOP_K)
    wts = jax.nn.softmax(top_v, axis=-1)
    comb = jnp.sum(jax.nn.one_hot(top_i, N_EXPERTS, dtype=jnp.float32) * wts[..., None],
                   axis=-2).astype(h.dtype)
    y = None
    for e in range(N_EXPERTS):
        ye = comb[..., e:e + 1] * swiglu(h, w1[e], w2[e])
        y = ye if y is None else y + ye
    return y


def run_trunk(x, c, caches, w):
    (w_ada, b_ada, g_mix, g_ffn, g_final, w_in, a_rel_bias, t5_bias, b_lambda, b_subln,
     c_q_norm, c_kv_norm, c_w_uq, c_w_uk, c_w_uv, w_gate, w_branch, w_out,
     ffn_w1, ffn_w2, moe_router, moe_w1, moe_w2) = w
    states = []
    for l in range(DEPTH):
        past = None if caches is None else tuple(cc[l] for cc in caches)
        mod = jax.nn.silu(c) @ w_ada[l] + b_ada[l]
        sh1, sc1, g1, sh2, sc2, g2 = jnp.split(mod[:, None, :], 6, axis=-1)
        h = rmsnorm(x, g_mix[l]) * (1.0 + sc1) + sh1
        outs, st = token_mixers(h, l, past, w_in, a_rel_bias, t5_bias, b_lambda, b_subln,
                                c_q_norm, c_kv_norm, c_w_uq, c_w_uk, c_w_uv)
        merged = None
        for i in range(N_BRANCH):
            term = jax.nn.sigmoid(h @ w_gate[l, i]) * (outs[i] @ w_branch[l, i])
            merged = term if merged is None else merged + term
        x = x + g1 * (merged @ w_out[l])
        h = rmsnorm(x, g_ffn[l]) * (1.0 + sc2) + sh2
        if l % 2 == 0:
            f = swiglu(h, ffn_w1[l // 2], ffn_w2[l // 2])
        else:
            f = moe_swiglu(h, moe_router[l // 2], moe_w1[l // 2], moe_w2[l // 2])
        x = x + g2 * f
        states.append(st)
    y = rmsnorm(x, g_final)
    new_state = tuple(jnp.stack([s[i] for s in states], axis=0) for i in range(6))
    return y, new_state


def setup_inputs(seed: int = 0) -> dict:
    key = jax.random.key(seed)
    ks = iter(jax.random.split(key, 48))
    D = D_MODEL
    n_dense = (DEPTH + 1) // 2
    n_moe = DEPTH // 2
    a_len = min(A_WINDOW, PAST_LEN)

    def nrm(shape, scale):
        return jax.random.normal(next(ks), shape, jnp.float32) * scale

    def gain(shape):
        return jnp.ones(shape, jnp.float32) + nrm(shape, 0.01)

    return {
        'x_prompt': nrm((BATCH, SEQ, D), 1.0),
        'x_sample': nrm((DEC_BATCH, DEC_SEQ, D), 1.0),
        'c_prompt': nrm((BATCH, D), 1.0),
        'c_sample': nrm((DEC_BATCH, D), 1.0),
        'cache_a_kv': nrm((DEPTH, DEC_BATCH, a_len, 2, H_A, DH_A), 1.0),
        'cache_b_kv': nrm((DEPTH, DEC_BATCH, PAST_LEN, 2, H_B, 2 * DH_B), 1.0),
        'cache_c_latent': nrm((DEPTH, DEC_BATCH, PAST_LEN, C_KV_RANK), 1.0),
        'cache_c_krope': nrm((DEPTH, DEC_BATCH, PAST_LEN, C_ROPE), 1.0),
        'cache_d_kv': nrm((DEPTH, DEC_BATCH, PAST_LEN, 2, H_D, DH_D), 1.0),
        'cache_d_kidx': nrm((DEPTH, DEC_BATCH, PAST_LEN, IDX_DIM), 1.0),
        'w_ada': nrm((DEPTH, D, 6 * D), 0.5 * D ** -0.5),
        'b_ada': nrm((DEPTH, 6 * D), 0.02),
        'g_mix': gain((DEPTH, D)),
        'g_ffn': gain((DEPTH, D)),
        'g_final': gain((D,)),
        'w_in': nrm((DEPTH, D, IN_COLS), D ** -0.5),
        'a_rel_bias': nrm((DEPTH, 2 * A_CLIP + 1, H_A), 0.2),
        't5_bias': nrm((T5_BUCKETS, H_B + H_D), 0.2),
        'b_lambda': nrm((DEPTH, 4, DH_B), 0.1),
        'b_subln': gain((DEPTH, 2 * DH_B)),
        'c_q_norm': gain((DEPTH, C_Q_RANK)),
        'c_kv_norm': gain((DEPTH, C_KV_RANK)),
        'c_w_uq': nrm((DEPTH, C_Q_RANK, H_C, C_NOPE + C_ROPE), C_Q_RANK ** -0.5),
        'c_w_uk': nrm((DEPTH, C_KV_RANK, H_C, C_NOPE), C_KV_RANK ** -0.5),
        'c_w_uv': nrm((DEPTH, C_KV_RANK, H_C, C_V), C_KV_RANK ** -0.5),
        'w_gate': nrm((DEPTH, N_BRANCH, D, D), D ** -0.5),
        'w_branch': nrm((DEPTH, N_BRANCH, BRANCH_W, D), BRANCH_W ** -0.5),
        'w_out': nrm((DEPTH, D, D), D ** -0.5),
        'ffn_w1': nrm((n_dense, D, 2 * D_FF), D ** -0.5),
        'ffn_w2': nrm((n_dense, D_FF, D), D_FF ** -0.5),
        'moe_router': nrm((n_moe, D, N_EXPERTS), D ** -0.5),
        'moe_w1': nrm((n_moe, N_EXPERTS, D, 2 * D_FF_E), D ** -0.5),
        'moe_w2': nrm((n_moe, N_EXPERTS, D_FF_E, D), D_FF_E ** -0.5),
    }


def reference(x_prompt, x_sample, c_prompt, c_sample,
              cache_a_kv, cache_b_kv, cache_c_latent, cache_c_krope, cache_d_kv, cache_d_kidx,
              w_ada, b_ada, g_mix, g_ffn, g_final, w_in, a_rel_bias, t5_bias, b_lambda, b_subln,
              c_q_norm, c_kv_norm, c_w_uq, c_w_uk, c_w_uv, w_gate, w_branch, w_out,
              ffn_w1, ffn_w2, moe_router, moe_w1, moe_w2):
    weights = (w_ada, b_ada, g_mix, g_ffn, g_final, w_in, a_rel_bias, t5_bias, b_lambda, b_subln,
               c_q_norm, c_kv_norm, c_w_uq, c_w_uk, c_w_uv, w_gate, w_branch, w_out,
               ffn_w1, ffn_w2, moe_router, moe_w1, moe_w2)
    y_prompt, st_p = run_trunk(x_prompt, c_prompt, None, weights)
    caches = (cache_a_kv, cache_b_kv, cache_c_latent, cache_c_krope, cache_d_kv, cache_d_kidx)
    y_sample, st_s = run_trunk(x_sample, c_sample, caches, weights)
    return (y_prompt, y_sample,
            st_p[0], st_p[1], st_p[2], st_p[3], st_p[4], st_p[5],
            st_s[0], st_s[1], st_s[2], st_s[3], st_s[4], st_s[5])
```

```python
import functools
import math

import numpy as np
import jax
import jax.numpy as jnp
from jax import lax
from jax.experimental import pallas as pl
from jax.experimental.pallas import tpu as pltpu

D_MODEL = 2048
DEPTH = 2
CHUNK = 64
EPS = 1e-6
NEG = -1e30
H_A, DH_A, A_BAND_CHUNKS, A_CLIP = 8, 64, 8, 256
A_WINDOW = A_BAND_CHUNKS * CHUNK
H_B, DH_B = 4, 64
H_C, C_NOPE, C_ROPE, C_V, C_Q_RANK, C_KV_RANK = 4, 64, 32, 128, 384, 256
ROPE_THETA = 10000.0
H_D, DH_D, IDX_HEADS, IDX_DIM, IDX_TOPK = 8, 64, 8, 64, 256
T5_BUCKETS, T5_MAX_DIST = 32, 128
N_BRANCH, BRANCH_W = 4, 512
N_EXPERTS, TOP_K = 8, 2

LANES = 128
VMEM_LIMIT = 56 * 1024 * 1024

COL_AQ, COL_AK, COL_AV = 0, 512, 1024
COL_BQ, COL_BK, COL_BV = 1536, 2048, 2560
COL_DQ, COL_DK, COL_DV, COL_DQI = 3072, 3584, 4096, 4608
COL_CQ = 5120
COL_DW = 5504
COL_CKV = 5632
COL_KR = 5888
COL_KI = 6016
PROJ_COLS = 6144

KV_TILE = 256
F32 = jnp.float32
BF16 = jnp.bfloat16
NT_DIMS = (((1,), (1,)), ((), ()))


def _cp(*sem):
    return pltpu.CompilerParams(dimension_semantics=sem, vmem_limit_bytes=VMEM_LIMIT)


def _dot(a, b):
    return jnp.dot(a, b, preferred_element_type=F32)


def _dot_nt(a, b):
    return lax.dot_general(a, b, NT_DIMS, preferred_element_type=F32)


def _ada_kernel(c_ref, w_ref, b_ref, o_ref):
    c = c_ref[...]
    a = (c * jax.nn.sigmoid(c)).astype(BF16)
    o_ref[...] = _dot(a, w_ref[...].astype(BF16)) + b_ref[...]


def _ada_mod(c_all, w, b):
    R, D = c_all.shape
    N = w.shape[1]
    tn = 1024
    return pl.pallas_call(
        _ada_kernel,
        grid=(N // tn,),
        in_specs=[pl.BlockSpec((R, D), lambda n: (0, 0)),
                  pl.BlockSpec((D, tn), lambda n: (0, n)),
                  pl.BlockSpec((1, tn), lambda n: (0, n))],
        out_specs=pl.BlockSpec((R, tn), lambda n: (0, n)),
        out_shape=jax.ShapeDtypeStruct((R, N), F32),
        compiler_params=_cp("arbitrary"),
    )(c_all, w, b)


def _norm_mod_kernel(x_ref, g_ref, sc_ref, sh_ref, o_ref):
    x = x_ref[...]
    y = x * lax.rsqrt(jnp.mean(x * x, axis=-1, keepdims=True) + EPS) * g_ref[...]
    o_ref[...] = (y * (1.0 + sc_ref[...]) + sh_ref[...]).astype(o_ref.dtype)


def _norm_kernel(x_ref, g_ref, o_ref):
    x = x_ref[...]
    y = x * lax.rsqrt(jnp.mean(x * x, axis=-1, keepdims=True) + EPS) * g_ref[...]
    o_ref[...] = y.astype(o_ref.dtype)


def _norm_mod(x3, gain, mod3, sc_blk, sh_blk):
    G, R, D = x3.shape
    gb = 8
    return pl.pallas_call(
        _norm_mod_kernel,
        grid=(G // gb,),
        in_specs=[pl.BlockSpec((gb, R, D), lambda g: (g, 0, 0)),
                  pl.BlockSpec((1, 1, D), lambda g: (0, 0, 0)),
                  pl.BlockSpec((gb, 1, D), lambda g: (g, 0, sc_blk)),
                  pl.BlockSpec((gb, 1, D), lambda g: (g, 0, sh_blk))],
        out_specs=pl.BlockSpec((gb, R, D), lambda g: (g, 0, 0)),
        out_shape=jax.ShapeDtypeStruct((G, R, D), BF16),
        compiler_params=_cp("parallel"),
    )(x3, gain.reshape(1, 1, D), mod3, mod3)


def _norm_final(x3, gain):
    G, R, D = x3.shape
    gb = 8
    return pl.pallas_call(
        _norm_kernel,
        grid=(G // gb,),
        in_specs=[pl.BlockSpec((gb, R, D), lambda g: (g, 0, 0)),
                  pl.BlockSpec((1, 1, D), lambda g: (0, 0, 0))],
        out_specs=pl.BlockSpec((gb, R, D), lambda g: (g, 0, 0)),
        out_shape=jax.ShapeDtypeStruct((G, R, D), F32),
        compiler_params=_cp("parallel"),
    )(x3, gain.reshape(1, 1, D))


def _proj_kernel(x_ref, w_ref, o32_ref, o16_ref):
    acc = _dot(x_ref[...], w_ref[...])
    o32_ref[...] = acc
    o16_ref[...] = acc.astype(BF16)


def _proj(h, w):
    M, K = h.shape
    N = w.shape[1]
    tm = 1024 if M % 1024 == 0 else M
    tn = 512
    return pl.pallas_call(
        _proj_kernel,
        grid=(M // tm, N // tn),
        in_specs=[pl.BlockSpec((tm, K), lambda m, n: (m, 0)),
                  pl.BlockSpec((K, tn), lambda m, n: (0, n))],
        out_specs=[pl.BlockSpec((tm, tn), lambda m, n: (m, n)),
                   pl.BlockSpec((tm, tn), lambda m, n: (m, n))],
        out_shape=[jax.ShapeDtypeStruct((M, N), F32), jax.ShapeDtypeStruct((M, N), BF16)],
        compiler_params=_cp("parallel", "arbitrary"),
    )(h, w)


def _merge_kernel(h_ref, wg_ref, o_ref, wb_ref, out_ref, acc_ref):
    i = pl.program_id(2)

    @pl.when(i == 0)
    def _():
        acc_ref[...] = jnp.zeros_like(acc_ref)

    gate = jax.nn.sigmoid(_dot(h_ref[...], wg_ref[0].astype(BF16)))
    acc_ref[...] += gate * _dot(o_ref[0], wb_ref[0].astype(BF16))

    @pl.when(i == pl.num_programs(2) - 1)
    def _():
        out_ref[...] = acc_ref[...].astype(out_ref.dtype)


def _merge(h, w_gate, outs, w_branch):
    M, D = h.shape
    nb, _, W = outs.shape
    tm = 1024 if M % 1024 == 0 else M
    tn = 512
    return pl.pallas_call(
        _merge_kernel,
        grid=(M // tm, D // tn, nb),
        in_specs=[pl.BlockSpec((tm, D), lambda m, n, i: (m, 0)),
                  pl.BlockSpec((1, D, tn), lambda m, n, i: (i, 0, n)),
                  pl.BlockSpec((1, tm, W), lambda m, n, i: (i, m, 0)),
                  pl.BlockSpec((1, W, tn), lambda m, n, i: (i, 0, n))],
        out_specs=pl.BlockSpec((tm, tn), lambda m, n, i: (m, n)),
        out_shape=jax.ShapeDtypeStruct((M, D), BF16),
        scratch_shapes=[pltpu.VMEM((tm, tn), F32)],
        compiler_params=_cp("parallel", "arbitrary", "arbitrary"),
    )(h, w_gate, outs, w_branch)


def _mm_res_kernel(a_ref, w_ref, x_ref, g_ref, o_ref, acc_ref):
    k = pl.program_id(2)

    @pl.when(k == 0)
    def _():
        acc_ref[...] = jnp.zeros_like(acc_ref)

    acc_ref[...] += _dot(a_ref[...], w_ref[...].astype(BF16))

    @pl.when(k == pl.num_programs(2) - 1)
    def _():
        gb, R, tn = x_ref.shape
        y = acc_ref[...].reshape(gb, R, tn)
        o_ref[...] = x_ref[...] + g_ref[...] * y


def _mm_res(a, w, x3, mod3, g_blk, tk):
    M, K = a.shape
    G, R, D = x3.shape
    tm = 1024 if M % 1024 == 0 else M
    tn = 1024
    gb = tm // R
    nblk = D // tn
    return pl.pallas_call(
        _mm_res_kernel,
        grid=(M // tm, nblk, K // tk),
        in_specs=[pl.BlockSpec((tm, tk), lambda m, n, k: (m, k)),
                  pl.BlockSpec((tk, tn), lambda m, n, k: (k, n)),
                  pl.BlockSpec((gb, R, tn), lambda m, n, k: (m, 0, n)),
                  pl.BlockSpec((gb, 1, tn), lambda m, n, k: (m, 0, g_blk * nblk + n))],
        out_specs=pl.BlockSpec((gb, R, tn), lambda m, n, k: (m, 0, n)),
        out_shape=jax.ShapeDtypeStruct((G, R, D), F32),
        scratch_shapes=[pltpu.VMEM((tm, tn), F32)],
        compiler_params=_cp("parallel", "arbitrary", "arbitrary"),
    )(a, w, x3, mod3)


def _swiglu_up_kernel(h_ref, wg_ref, wu_ref, comb_ref, o_ref, wg16, wu16, *, use_comb):
    e = pl.program_id(0)
    m = pl.program_id(2)

    @pl.when(m == 0)
    def _():
        wg16[...] = wg_ref[0].astype(BF16)
        wu16[...] = wu_ref[0].astype(BF16)

    h = h_ref[...]
    g = _dot(h, wg16[...])
    u = _dot(h, wu16[...])
    act = g * jax.nn.sigmoid(g) * u
    if use_comb:
        comb = comb_ref[...]
        col = lax.broadcasted_iota(jnp.int32, comb.shape, 1)
        act = act * jnp.sum(jnp.where(col == e, comb, 0.0), axis=-1, keepdims=True)
    o_ref[...] = act.astype(o_ref.dtype)


def _swiglu_up(h, w1, comb, use_comb):
    M, D = h.shape
    E, _, F2 = w1.shape
    F = F2 // 2
    tm = 1024 if M % 1024 == 0 else M
    tn = 512
    nf = F // tn
    kern = functools.partial(_swiglu_up_kernel, use_comb=use_comb)
    return pl.pallas_call(
        kern,
        grid=(E, nf, M // tm),
        in_specs=[pl.BlockSpec((tm, D), lambda e, n, m: (m, 0)),
                  pl.BlockSpec((1, D, tn), lambda e, n, m: (e, 0, n)),
                  pl.BlockSpec((1, D, tn), lambda e, n, m: (e, 0, nf + n)),
                  pl.BlockSpec((tm, comb.shape[1]), lambda e, n, m: (m, 0))],
        out_specs=pl.BlockSpec((tm, tn), lambda e, n, m: (m, e * nf + n)),
        out_shape=jax.ShapeDtypeStruct((M, E * F), BF16),
        scratch_shapes=[pltpu.VMEM((D, tn), BF16), pltpu.VMEM((D, tn), BF16)],
        compiler_params=_cp("arbitrary", "arbitrary", "arbitrary"),
    )(h, w1, w1, comb)


def _router_kernel(x_ref, g_ref, sc_ref, sh_ref, w_ref, o_ref):
    x = x_ref[...]
    y = x * lax.rsqrt(jnp.mean(x * x, axis=-1, keepdims=True) + EPS) * g_ref[...]
    h = (y * (1.0 + sc_ref[...]) + sh_ref[...]).reshape(-1, x.shape[-1])
    logits = jnp.dot(h, w_ref[...], preferred_element_type=F32, precision=lax.Precision.HIGHEST)
    col = lax.broadcasted_iota(jnp.int32, logits.shape, 1)
    logits = jnp.where(col < N_EXPERTS, logits, -jnp.inf)
    m1 = jnp.max(logits, axis=-1, keepdims=True)
    i1 = jnp.min(jnp.where(logits == m1, col, LANES), axis=-1, keepdims=True)
    rest = jnp.where(col == i1, -jnp.inf, logits)
    m2 = jnp.max(rest, axis=-1, keepdims=True)
    i2 = jnp.min(jnp.where(rest == m2, col, LANES), axis=-1, keepdims=True)
    e2 = jnp.exp(m2 - m1)
    w1 = 1.0 / (1.0 + e2)
    w2 = e2 / (1.0 + e2)
    o_ref[...] = jnp.where(col == i1, w1, 0.0) + jnp.where(col == i2, w2, 0.0)


def _router(x3, gain, mod3, sc_blk, sh_blk, w):
    G, R, D = x3.shape
    gb = 8
    wp = jnp.concatenate([w, jnp.zeros((D, LANES - w.shape[1]), w.dtype)], axis=1)
    return pl.pallas_call(
        _router_kernel,
        grid=(G // gb,),
        in_specs=[pl.BlockSpec((gb, R, D), lambda g: (g, 0, 0)),
                  pl.BlockSpec((1, 1, D), lambda g: (0, 0, 0)),
                  pl.BlockSpec((gb, 1, D), lambda g: (g, 0, sc_blk)),
                  pl.BlockSpec((gb, 1, D), lambda g: (g, 0, sh_blk)),
                  pl.BlockSpec((D, LANES), lambda g: (0, 0))],
        out_specs=pl.BlockSpec((gb * R, LANES), lambda g: (g, 0)),
        out_shape=jax.ShapeDtypeStruct((G * R, LANES), F32),
        compiler_params=_cp("parallel"),
    )(x3, gain.reshape(1, 1, D), mod3, mod3, wp)


def _band_kernel(q_ref, kp_ref, kc_ref, vp_ref, vc_ref, bias_ref, o_ref, kwin, vwin, *,
                 qblk, first_prev_invalid):
    i = pl.program_id(1)
    kwin[0:A_WINDOW, :] = kp_ref[0].astype(BF16)
    kwin[A_WINDOW:A_WINDOW + qblk, :] = kc_ref[0].astype(BF16)
    vwin[0:A_WINDOW, :] = vp_ref[0].astype(BF16)
    vwin[A_WINDOW:A_WINDOW + qblk, :] = vc_ref[0].astype(BF16)
    wlen = A_WINDOW + CHUNK
    low = lax.broadcasted_iota(jnp.int32, (1, LANES), 1) < DH_A
    kcol = lax.broadcasted_iota(jnp.int32, (1, wlen), 1)
    prev_ok = (i > 0) if first_prev_invalid else True

    def chunk_body(c, carry):
        r0 = pl.multiple_of(c * CHUNK, CHUNK)
        valid = jnp.logical_or(kcol + r0 >= A_WINDOW, prev_ok)
        for p in range(H_A // 2):
            cols = slice(p * LANES, (p + 1) * LANES)
            qp = q_ref[0, pl.ds(r0, CHUNK), cols]
            kt = kwin[pl.ds(r0, wlen), cols]
            vt = vwin[pl.ds(r0, wlen), cols]
            halves = []
            for hh in range(2):
                qm = jnp.where(low if hh == 0 else jnp.logical_not(low), qp, jnp.zeros_like(qp))
                s = _dot_nt(qm, kt) * (DH_A ** -0.5) + bias_ref[2 * p + hh]
                s = jnp.where(valid, s, NEG)
                m = jnp.max(s, axis=-1, keepdims=True)
                e = jnp.exp(s - m)
                l = jnp.sum(e, axis=-1, keepdims=True)
                halves.append(_dot(e.astype(BF16), vt) / l)
            o_ref[0, pl.ds(r0, CHUNK), cols] = jnp.where(low, halves[0], halves[1]).astype(o_ref.dtype)
        return carry

    lax.fori_loop(0, qblk // CHUNK, chunk_body, 0)


def _band_attention(q_arr, q_col, kp_arr, kp_col, kc_arr, kc_col, vp_arr, vp_col, vc_arr, vc_col,
                    bias, nb, T, qblk, prompt, row0):
    nq = T // qblk
    W = H_A * DH_A
    cb = lambda col: col // W
    r0 = row0 // qblk

    def tok_spec(col):
        return pl.BlockSpec((1, qblk, W), lambda b, i: (0, r0 + b * nq + i, cb(col)))

    if prompt:
        def prev_spec(col):
            return pl.BlockSpec((1, A_WINDOW, W),
                                lambda b, i: (0, r0 + b * nq + jnp.maximum(i - 1, 0), cb(col)))
    else:
        def prev_spec(col):
            return pl.BlockSpec((1, A_WINDOW, W), lambda b, i: (b, 0, cb(col)))

    kern = functools.partial(_band_kernel, qblk=qblk, first_prev_invalid=prompt)
    return pl.pallas_call(
        kern,
        grid=(nb, nq),
        in_specs=[tok_spec(q_col), prev_spec(kp_col), tok_spec(kc_col), prev_spec(vp_col), tok_spec(vc_col),
                  pl.BlockSpec(bias.shape, lambda b, i: (0, 0, 0))],
        out_specs=pl.BlockSpec((1, qblk, W), lambda b, i: (b * nq + i, 0, 0)),
        out_shape=jax.ShapeDtypeStruct((nb * nq, qblk, W), BF16),
        scratch_shapes=[pltpu.VMEM((A_WINDOW + qblk, W), BF16), pltpu.VMEM((A_WINDOW + qblk, W), BF16)],
        compiler_params=_cp("parallel", "arbitrary"),
    )(q_arr, kp_arr, kc_arr, vp_arr, vc_arr, bias)


def _tile_class(j, jd):
    return jnp.clip(j - jd + 2, 0, 2)


def _online_update(s, v_tile, m_ref, l_ref, acc_ref, idx):
    m_old = m_ref[idx]
    m_new = jnp.maximum(m_old, jnp.max(s, axis=-1, keepdims=True))
    alpha = jnp.exp(m_old - m_new)
    p = jnp.exp(s - m_new)
    l_ref[idx] = alpha * l_ref[idx] + jnp.sum(p, axis=-1, keepdims=True)
    acc_ref[idx] = alpha * acc_ref[idx] + _dot(p.astype(BF16), v_tile)
    m_ref[idx] = m_new


def _init_state(m_ref, l_ref, acc_ref):
    m_ref[...] = jnp.full(m_ref.shape, -3e38, F32)
    l_ref[...] = jnp.zeros(l_ref.shape, F32)
    acc_ref[...] = jnp.zeros(acc_ref.shape, F32)


def _diff_kernel(lam_ref, q_ref, k_ref, v_ref, tab_ref, g_ref, o_ref, m_ref, l_ref, acc_ref, *,
                 jd0, dj, out_scale):
    i = pl.program_id(1)
    jd = jd0 + i * dj
    _init_state(m_ref, l_ref, acc_ref)
    low = lax.broadcasted_iota(jnp.int32, (1, LANES), 1) < DH_B
    W = 2 * DH_B

    def tile_body(j, carry):
        k0 = pl.multiple_of(j * KV_TILE, KV_TILE)
        cls = _tile_class(j, jd)
        for h in range(H_B):
            cols = slice(h * W, (h + 1) * W)
            qh = q_ref[0, :, cols]
            kt = k_ref[0, pl.ds(k0, KV_TILE), cols]
            vt = v_ref[0, pl.ds(k0, KV_TILE), cols]
            bias = tab_ref[h, cls]
            for c in range(2):
                qm = jnp.where(low if c == 0 else jnp.logical_not(low), qh, jnp.zeros_like(qh))
                s = _dot_nt(qm, kt) * (DH_B ** -0.5) + bias
                _online_update(s, vt, m_ref, l_ref, acc_ref, 2 * h + c)
        return carry

    lax.fori_loop(0, jd + 1, tile_body, 0)
    lam = lam_ref[0]
    for h in range(H_B):
        o = acc_ref[2 * h] / l_ref[2 * h] - lam * (acc_ref[2 * h + 1] / l_ref[2 * h + 1])
        y = o * lax.rsqrt(jnp.mean(o * o, axis=-1, keepdims=True) + EPS) * g_ref[...]
        o_ref[0, :, h * W:(h + 1) * W] = (y * out_scale).astype(o_ref.dtype)


def _tok3(arr):
    return arr.reshape(1, arr.shape[0], arr.shape[1])


def _q_spec(T, QB, width, col, row0):
    nq = T // QB
    r0 = row0 // QB
    return pl.BlockSpec((1, QB, width), lambda b, i: (0, r0 + b * nq + i, col // width))


def _kv_spec(prompt, Tk, width, col):
    if prompt:
        return pl.BlockSpec((1, Tk, width), lambda b, i: (0, b, col // width))
    return pl.BlockSpec((1, Tk, width), lambda b, i: (b, 0, col // width))


def _diff_attention(lam, q_arr, q_col, k_arr, k_col, v_arr, v_col, tab, subln, nb, T, Tk, QB, prompt,
                    row0, jd0, dj, out_scale):
    W = H_B * 2 * DH_B
    nq = T // QB
    q_spec = _q_spec(T, QB, W, q_col, row0)
    q_view = _tok3(q_arr)
    k_view = _tok3(k_arr) if prompt else k_arr
    v_view = _tok3(v_arr) if prompt else v_arr
    kspec = _kv_spec(prompt, Tk, W, k_col)
    vspec = _kv_spec(prompt, Tk, W, v_col)
    kern = functools.partial(_diff_kernel, jd0=jd0, dj=dj, out_scale=out_scale)
    return pl.pallas_call(
        kern,
        grid=(nb, nq),
        in_specs=[pl.BlockSpec(memory_space=pltpu.SMEM),
                  q_spec, kspec, vspec,
                  pl.BlockSpec(tab.shape, lambda b, i: (0, 0, 0, 0)),
                  pl.BlockSpec((1, 2 * DH_B), lambda b, i: (0, 0))],
        out_specs=pl.BlockSpec((1, QB, W), lambda b, i: (b * nq + i, 0, 0)),
        out_shape=jax.ShapeDtypeStruct((nb * nq, QB, W), BF16),
        scratch_shapes=[pltpu.VMEM((2 * H_B, QB, 1), F32), pltpu.VMEM((2 * H_B, QB, 1), F32),
                        pltpu.VMEM((2 * H_B, QB, 2 * DH_B), F32)],
        compiler_params=_cp("parallel", "arbitrary"),
    )(lam, q_view, k_view, v_view, tab, subln)


C_KW = C_KV_RANK + LANES


def _mla_prep_kernel(cq_ref, ckv_ref, kr_ref, qcs_ref, kcc_ref, kss_ref, gq_ref, gkv_ref, wuq_ref, wuk_ref,
                     q_ref, kcat_ref, lat_ref, krn_ref):
    cq = cq_ref[:, :C_Q_RANK]
    cqn = cq * lax.rsqrt(jnp.mean(cq * cq, axis=-1, keepdims=True) + EPS) * gq_ref[...]
    cqh = _dot(cqn.astype(BF16), wuq_ref[...])
    scale = (C_NOPE + C_ROPE) ** -0.5
    qcs = qcs_ref[...]
    for h in range(H_C):
        blk = cqh[:, h * LANES:(h + 1) * LANES]
        q_lat = _dot(blk.astype(BF16), wuk_ref[h])
        q_ref[h, :, :C_KV_RANK] = (q_lat * scale).astype(q_ref.dtype)
        q_ref[h, :, C_KV_RANK:] = (blk * qcs * scale).astype(q_ref.dtype)
    ckv = ckv_ref[...]
    lat = ckv * lax.rsqrt(jnp.mean(ckv * ckv, axis=-1, keepdims=True) + EPS) * gkv_ref[...]
    lat_ref[...] = lat
    kr = kr_ref[...]
    krn = kr * kcc_ref[...] + pltpu.roll(kr, 2 * C_ROPE, 1) * kss_ref[...]
    krn_ref[...] = krn
    kcat_ref[:, :C_KV_RANK] = lat.astype(kcat_ref.dtype)
    kcat_ref[:, C_KV_RANK:] = krn.astype(kcat_ref.dtype)


def _mla_prep(proj32, qcs, kcc, kss, gq, gkv, wuq, wuk):
    M = proj32.shape[0]
    tm = 512 if M % 512 == 0 else M
    row = lambda w, col: pl.BlockSpec((tm, w), lambda m: (m, col // w))
    full = lambda a: pl.BlockSpec(a.shape, lambda m: (0,) * a.ndim)
    return pl.pallas_call(
        _mla_prep_kernel,
        grid=(M // tm,),
        in_specs=[row(512, COL_CQ), row(C_KV_RANK, COL_CKV), row(LANES, COL_KR),
                  row(LANES, 0), row(LANES, 0), row(LANES, 0),
                  full(gq), full(gkv), full(wuq), full(wuk)],
        out_specs=[pl.BlockSpec((H_C, tm, C_KW), lambda m: (0, m, 0)),
                   pl.BlockSpec((tm, C_KW), lambda m: (m, 0)),
                   pl.BlockSpec((tm, C_KV_RANK), lambda m: (m, 0)),
                   pl.BlockSpec((tm, LANES), lambda m: (m, 0))],
        out_shape=[jax.ShapeDtypeStruct((H_C, M, C_KW), BF16),
                   jax.ShapeDtypeStruct((M, C_KW), BF16),
                   jax.ShapeDtypeStruct((M, C_KV_RANK), F32),
                   jax.ShapeDtypeStruct((M, LANES), F32)],
        compiler_params=_cp("parallel"),
    )(proj32, proj32, proj32, qcs, kcc, kss, gq, gkv, wuq, wuk)


def _mla_kernel(q_ref, k_ref, tab_ref, wuv_ref, o_ref, m_ref, l_ref, acc_ref, *, jd0, dj, QB):
    i = pl.program_id(1)
    jd = jd0 + i * dj
    _init_state(m_ref, l_ref, acc_ref)
    q = q_ref[...].reshape(H_C * QB, C_KW)

    def tile_body(j, carry):
        k0 = pl.multiple_of(j * KV_TILE, KV_TILE)
        kt = k_ref[0, pl.ds(k0, KV_TILE), :]
        mask = tab_ref[_tile_class(j, jd)]
        s = _dot_nt(q, kt).reshape(H_C, QB, KV_TILE) + mask[None]
        _online_update(s.reshape(H_C * QB, KV_TILE), kt[:, :C_KV_RANK], m_ref, l_ref, acc_ref, 0)
        return carry

    lax.fori_loop(0, jd + 1, tile_body, 0)
    o = (acc_ref[0] / l_ref[0]).astype(BF16)
    for h in range(H_C):
        o_ref[0, :, h * C_V:(h + 1) * C_V] = _dot(o[h * QB:(h + 1) * QB], wuv_ref[h]).astype(o_ref.dtype)


def _mla_attention(q_arr, k_arr, tab, wuv, nb, T, Tk, QB, prompt, row0, jd0, dj):
    nq = T // QB
    r0 = row0 // QB
    k_view = _tok3(k_arr) if prompt else k_arr
    kspec = _kv_spec(prompt, Tk, C_KW, 0)
    kern = functools.partial(_mla_kernel, jd0=jd0, dj=dj, QB=QB)
    return pl.pallas_call(
        kern,
        grid=(nb, nq),
        in_specs=[pl.BlockSpec((H_C, QB, C_KW), lambda b, i: (0, r0 + b * nq + i, 0)),
                  kspec,
                  pl.BlockSpec(tab.shape, lambda b, i: (0, 0, 0)),
                  pl.BlockSpec(wuv.shape, lambda b, i: (0, 0, 0))],
        out_specs=pl.BlockSpec((1, QB, H_C * C_V), lambda b, i: (b * nq + i, 0, 0)),
        out_shape=jax.ShapeDtypeStruct((nb * nq, QB, H_C * C_V), BF16),
        scratch_shapes=[pltpu.VMEM((1, H_C * QB, 1), F32), pltpu.VMEM((1, H_C * QB, 1), F32),
                        pltpu.VMEM((1, H_C * QB, C_KV_RANK), F32)],
        compiler_params=_cp("parallel", "arbitrary"),
    )(q_arr, k_view, tab, wuv)


INT_MIN = -2 ** 31


def _dsa_kernel(q_ref, qi_ref, dw_ref, k_ref, v_ref, ki_ref, tab_ref, tri_ref, o_ref,
                key_ref, thr_ref, need_ref, run_ref, m_ref, l_ref, acc_ref, *, jd0, dj, QB, RB, topk):
    i = pl.program_id(1)
    jd = jd0 + i * dj
    nkv = jd + 1
    low = lax.broadcasted_iota(jnp.int32, (1, LANES), 1) < DH_D
    NP = H_D // 2

    qrow = lax.broadcasted_iota(jnp.int32, (QB, KV_TILE), 0) // CHUNK
    kcolc = lax.broadcasted_iota(jnp.int32, (QB, KV_TILE), 1) // CHUNK
    diag_adm = kcolc <= qrow

    def score_body(j, carry):
        k0 = pl.multiple_of(j * KV_TILE, KV_TILE)
        kid = ki_ref[0, pl.ds(k0, KV_TILE), :]
        sc = jnp.zeros((QB, KV_TILE), F32)
        for p in range(NP):
            qp = qi_ref[0, :, p * LANES:(p + 1) * LANES]
            for hh in range(2):
                h = 2 * p + hh
                qm = jnp.where(low if hh == 0 else jnp.logical_not(low), qp, jnp.zeros_like(qp))
                a = jnp.maximum(_dot_nt(qm, kid) * (IDX_DIM ** -0.5), 0.0)
                sc = sc + (dw_ref[0, :, h:h + 1] * (IDX_HEADS ** -0.5)) * a
        bits = pltpu.bitcast(sc, jnp.int32)
        key = jnp.where(bits < 0, bits ^ 0x7FFFFFFF, bits)
        key = jnp.where(sc == 0.0, 0, key)
        adm = jnp.logical_or(j < jd, diag_adm)
        key_ref[j] = jnp.where(adm, key, INT_MIN)
        return carry

    lax.fori_loop(0, nkv, score_body, 0)

    def count_ge(r0, cand, strict):
        def body(j, part):
            for c in range(KV_TILE // LANES):
                kk = key_ref[j, r0:r0 + RB, c * LANES:(c + 1) * LANES]
                hit = (kk > cand) if strict else (kk >= cand)
                part = part + jnp.where(hit, 1.0, 0.0)
            return part
        part = lax.fori_loop(0, nkv, body, jnp.zeros((RB, LANES), F32))
        return jnp.sum(part, axis=-1, keepdims=True)

    kf = float(topk)
    for rb in range(QB // RB):
        r0 = rb * RB
        c0 = count_ge(r0, jnp.zeros((RB, 1), jnp.int32), False)
        t0 = jnp.where(c0 >= kf, 0, INT_MIN).astype(jnp.int32)

        def bit_body(b, t):
            cand = t | jnp.left_shift(jnp.int32(1), 30 - b)
            cnt = count_ge(r0, cand, False)
            return jnp.where(cnt >= kf, cand, t)

        t = lax.fori_loop(0, 31, bit_body, t0)
        thr_ref[r0:r0 + RB, :] = t
        need_ref[r0:r0 + RB, :] = kf - count_ge(r0, t, True)

    _init_state(m_ref, l_ref, acc_ref)
    run_ref[...] = jnp.zeros(run_ref.shape, F32)
    thr = thr_ref[...]
    need = need_ref[...]

    def attn_body(j, carry):
        k0 = pl.multiple_of(j * KV_TILE, KV_TILE)
        cls = _tile_class(j, jd)
        key = key_ref[j]
        eq = key == thr
        eqf = jnp.where(eq, 1.0, 0.0)
        before = _dot(eqf.astype(BF16), tri_ref[...]) + run_ref[...]
        run_ref[...] += jnp.sum(eqf, axis=-1, keepdims=True)
        sel = jnp.logical_or(key > thr, jnp.logical_and(eq, before < need))
        selb = jnp.where(sel, 0.0, NEG)
        for p in range(NP):
            cols = slice(p * LANES, (p + 1) * LANES)
            qp = q_ref[0, :, cols]
            kt = k_ref[0, pl.ds(k0, KV_TILE), cols]
            vt = v_ref[0, pl.ds(k0, KV_TILE), cols]
            for hh in range(2):
                h = 2 * p + hh
                qm = jnp.where(low if hh == 0 else jnp.logical_not(low), qp, jnp.zeros_like(qp))
                s = _dot_nt(qm, kt) * (DH_D ** -0.5) + tab_ref[h, cls] + selb
                _online_update(s, vt, m_ref, l_ref, acc_ref, h)
        return carry

    lax.fori_loop(0, nkv, attn_body, 0)
    for p in range(NP):
        o0 = acc_ref[2 * p] / l_ref[2 * p]
        o1 = acc_ref[2 * p + 1] / l_ref[2 * p + 1]
        o_ref[0, :, p * LANES:(p + 1) * LANES] = jnp.where(low, o0, o1).astype(o_ref.dtype)


def _dsa_attention(tok_arr, k_arr, v_arr, ki_arr, tab, tri, nb, T, Tk, QB, prompt, row0, jd0, dj, topk):
    proj16, proj32 = tok_arr
    W = H_D * DH_D
    nq = T // QB
    tok = lambda w, col: _q_spec(T, QB, w, col, row0)
    p16 = _tok3(proj16)
    p32 = _tok3(proj32)
    if prompt:
        k_view, v_view, ki_view = p16, p16, p16
        kspec, vspec, kispec = (_kv_spec(True, Tk, W, COL_DK), _kv_spec(True, Tk, W, COL_DV),
                                _kv_spec(True, Tk, LANES, COL_KI))
    else:
        k_view, v_view, ki_view = k_arr, v_arr, ki_arr
        kspec, vspec, kispec = _kv_spec(False, Tk, W, 0), _kv_spec(False, Tk, W, 0), _kv_spec(False, Tk, LANES, 0)
    RB = min(QB, 128)
    kern = functools.partial(_dsa_kernel, jd0=jd0, dj=dj, QB=QB, RB=RB, topk=topk)
    return pl.pallas_call(
        kern,
        grid=(nb, nq),
        in_specs=[tok(W, COL_DQ), tok(W, COL_DQI), tok(LANES, COL_DW), kspec, vspec, kispec,
                  pl.BlockSpec(tab.shape, lambda b, i: (0, 0, 0, 0)),
                  pl.BlockSpec(tri.shape, lambda b, i: (0, 0))],
        out_specs=pl.BlockSpec((1, QB, W), lambda b, i: (b * nq + i, 0, 0)),
        out_shape=jax.ShapeDtypeStruct((nb * nq, QB, W), BF16),
        scratch_shapes=[pltpu.VMEM((Tk // KV_TILE, QB, KV_TILE), jnp.int32),
                        pltpu.VMEM((QB, 1), jnp.int32), pltpu.VMEM((QB, 1), F32), pltpu.VMEM((QB, 1), F32),
                        pltpu.VMEM((H_D, QB, 1), F32), pltpu.VMEM((H_D, QB, 1), F32),
                        pltpu.VMEM((H_D, QB, LANES), F32)],
        compiler_params=_cp("parallel", "arbitrary"),
    )(p16, p16, p32, k_view, v_view, ki_view, tab, tri)


def _t5_bucket(rel):
    nb = T5_BUCKETS // 2
    max_exact = nb // 2
    ret = jnp.where(rel > 0, nb, 0)
    n = jnp.abs(rel)
    nf = jnp.maximum(n, 1).astype(F32)
    large = max_exact + (jnp.log(nf / max_exact) / math.log(T5_MAX_DIST / max_exact)
                         * (nb - max_exact)).astype(jnp.int32)
    large = jnp.minimum(large, nb - 1)
    return ret + jnp.where(n < max_exact, n, large)


def _tile_tables(QB, tab):
    qq = jnp.arange(QB)[:, None]
    kk = jnp.arange(KV_TILE)[None, :]
    tiles = []
    for cls in range(3):
        rel = (cls - 2) * KV_TILE + kk - qq
        adm = (kk // CHUNK <= qq // CHUNK) if cls == 2 else jnp.ones((QB, KV_TILE), bool)
        if tab is None:
            t = jnp.where(adm, 0.0, NEG).astype(F32)
        else:
            t = jnp.where(adm[None], jnp.transpose(tab[_t5_bucket(rel)], (2, 0, 1)), NEG).astype(F32)
        tiles.append(t)
    return jnp.stack(tiles, axis=0 if tab is None else 1)


def _band_bias(rel_table):
    qq = jnp.arange(CHUNK)[:, None]
    kk = jnp.arange(A_WINDOW + CHUNK)[None, :]
    rel = jnp.clip(kk - A_WINDOW - qq, -A_CLIP, A_CLIP) + A_CLIP
    return jnp.transpose(rel_table[rel], (2, 0, 1)).astype(F32)


def _rot_cols(w):
    half = w.shape[-1] // 2
    return jnp.concatenate([-w[..., half:], w[..., :half]], axis=-1)


def _padded_w_in(w):
    sizes = [512] * 6 + [C_Q_RANK, C_KV_RANK, C_ROPE] + [512] * 4 + [IDX_DIM, IDX_HEADS]
    offs = np.concatenate([[0], np.cumsum(sizes)])
    seg = [w[:, offs[i]:offs[i + 1]] for i in range(len(sizes))]
    (a_q, a_k, a_v, b_q, b_k, b_v, c_cq, c_ckv, c_kr, d_q, d_k, d_v, d_qi, d_ki, d_w) = seg
    z = lambda n: jnp.zeros((w.shape[0], n), w.dtype)
    rot = _rot_cols(c_kr)
    cols = [a_q, a_k, a_v, b_q, b_k, b_v, d_q, d_k, d_v, d_qi,
            c_cq, d_w, z(LANES - IDX_HEADS), c_ckv, c_kr, c_kr, rot, rot, d_ki, d_ki]
    out = jnp.concatenate(cols, axis=1).astype(BF16)
    assert out.shape[1] == PROJ_COLS
    return out


def _mla_weights(w_uq, w_uk, w_uv):
    blocks = []
    for h in range(H_C):
        nope = w_uq[:, h, :C_NOPE]
        rope = w_uq[:, h, C_NOPE:]
        blocks += [rope, _rot_cols(rope), nope]
    wuq = jnp.concatenate(blocks, axis=1).astype(BF16)
    wuk = jnp.transpose(w_uk, (1, 2, 0))
    wuk = jnp.concatenate([jnp.zeros((H_C, 2 * C_ROPE, C_KV_RANK), wuk.dtype), wuk], axis=1).astype(BF16)
    wuv = jnp.transpose(w_uv, (1, 0, 2)).astype(BF16)
    return wuq, wuk, wuv


def _rope_tables(pos):
    half = C_ROPE // 2
    freqs = ROPE_THETA ** (-jnp.arange(half, dtype=F32) / half)
    ang = pos.astype(F32)[:, None] * freqs
    cos = jnp.concatenate([jnp.cos(ang)] * 2, axis=-1)
    sin = jnp.concatenate([jnp.sin(ang)] * 2, axis=-1)
    z = jnp.zeros((pos.shape[0], 2 * C_ROPE), F32)
    qcs = jnp.concatenate([cos, sin, z], axis=-1)
    kcc = jnp.concatenate([cos, cos, z], axis=-1)
    kss = jnp.concatenate([sin, sin, z], axis=-1)
    return qcs, kcc, kss


def kernel(x_prompt, x_sample, c_prompt, c_sample, cache_a_kv, cache_b_kv, cache_c_latent, cache_c_krope, cache_d_kv, cache_d_kidx, w_ada, b_ada, g_mix, g_ffn, g_final, w_in, a_rel_bias, t5_bias, b_lambda, b_subln, c_q_norm, c_kv_norm, c_w_uq, c_w_uk, c_w_uv, w_gate, w_branch, w_out, ffn_w1, ffn_w2, moe_router, moe_w1, moe_w2):
    Bp, Tp, D = x_prompt.shape
    Bs, Ts, _ = x_sample.shape
    past = cache_b_kv.shape[2]
    a_len = cache_a_kv.shape[2]
    NPT, NST = Bp * Tp, Bs * Ts
    M = NPT + NST
    G = M // CHUNK
    assert Ts == CHUNK and a_len == A_WINDOW and past % KV_TILE == 0 and Tp % 512 == 0

    x3 = jnp.concatenate([x_prompt.reshape(NPT, D), x_sample.reshape(NST, D)], axis=0).reshape(G, CHUNK, D)
    grp_row = jnp.concatenate([jnp.repeat(jnp.arange(Bp), Tp // CHUNK), Bp + jnp.arange(Bs)])
    R = Bp + Bs
    Rpad = -(-R // 8) * 8
    c_all = jnp.concatenate([c_prompt, c_sample, jnp.zeros((Rpad - R, D), F32)], axis=0)

    QBP, QBS = KV_TILE, CHUNK
    Tks = past + KV_TILE
    jd_s = past // KV_TILE
    pos_all = jnp.concatenate([jnp.tile(jnp.arange(Tp), Bp), jnp.tile(past + jnp.arange(Ts), Bs)])
    qcs, kcc, kss = _rope_tables(pos_all)
    tab_b_p, tab_b_s = _tile_tables(QBP, t5_bias[:, :H_B]), _tile_tables(QBS, t5_bias[:, :H_B])
    tab_d_p, tab_d_s = _tile_tables(QBP, t5_bias[:, H_B:]), _tile_tables(QBS, t5_bias[:, H_B:])
    tab_c_p, tab_c_s = _tile_tables(QBP, None), _tile_tables(QBS, None)
    tri = (jnp.arange(KV_TILE)[:, None] < jnp.arange(KV_TILE)[None, :]).astype(BF16)
    topk_p = min(IDX_TOPK, Tp // 4)
    topk_s = min(IDX_TOPK, (past + Ts) // 4)

    def pad_keys(cache_part, new_part):
        z = jnp.zeros((Bs, Tks - past - Ts, new_part.shape[-1]), BF16)
        return jnp.concatenate([cache_part.astype(BF16), new_part.astype(BF16), z], axis=1)

    states = []
    for l in range(DEPTH):
        mod = _ada_mod(c_all, w_ada[l], b_ada[l].reshape(1, -1))
        mod3 = mod[grp_row].reshape(G, 1, 6 * D)
        h = _norm_mod(x3, g_mix[l], mod3, 1, 0).reshape(M, D)
        proj32, proj16 = _proj(h, _padded_w_in(w_in[l]))
        s16 = proj16[NPT:].reshape(Bs, Ts, PROJ_COLS)

        bias_a = _band_bias(a_rel_bias[l])
        p16v = proj16.reshape(1, M, PROJ_COLS)
        a_p = _band_attention(p16v, COL_AQ, p16v, COL_AK, p16v, COL_AK, p16v, COL_AV, p16v, COL_AV,
                              bias_a, Bp, Tp, 512, True, 0)
        ca = cache_a_kv[l].reshape(Bs, a_len, 2, H_A * DH_A)
        a_s = _band_attention(p16v, COL_AQ, ca[:, :, 0], 0, p16v, COL_AK, ca[:, :, 1], 0, p16v, COL_AV,
                              bias_a, Bs, Ts, CHUNK, False, NPT)
        a_out = jnp.concatenate([a_p.reshape(NPT, -1), a_s.reshape(NST, -1)], axis=0)

        lam_init = 0.8 - 0.6 * math.exp(-0.3 * l)
        lq1, lk1, lq2, lk2 = b_lambda[l].astype(F32)
        lam = (jnp.exp(jnp.sum(lq1 * lk1)) - jnp.exp(jnp.sum(lq2 * lk2)) + lam_init).reshape(1)
        subln = b_subln[l].reshape(1, -1)
        b_p = _diff_attention(lam, proj16, COL_BQ, proj16, COL_BK, proj16, COL_BV, tab_b_p, subln,
                              Bp, Tp, Tp, QBP, True, 0, 0, 1, 1.0 - lam_init)
        cb = cache_b_kv[l].reshape(Bs, past, 2, H_B * 2 * DH_B)
        kb_s = pad_keys(cb[:, :, 0], s16[:, :, COL_BK:COL_BK + 512])
        vb_s = pad_keys(cb[:, :, 1], s16[:, :, COL_BV:COL_BV + 512])
        b_s = _diff_attention(lam, proj16, COL_BQ, kb_s, 0, vb_s, 0, tab_b_s, subln,
                              Bs, Ts, Tks, QBS, False, NPT, jd_s, 0, 1.0 - lam_init)
        b_out = jnp.concatenate([b_p.reshape(NPT, -1), b_s.reshape(NST, -1)], axis=0)

        wuq, wuk, wuv = _mla_weights(c_w_uq[l], c_w_uk[l], c_w_uv[l])
        q_c, kcat, lat_new, kr_new = _mla_prep(proj32, qcs, kcc, kss, c_q_norm[l].reshape(1, -1),
                                               c_kv_norm[l].reshape(1, -1), wuq, wuk)
        c_p = _mla_attention(q_c, kcat, tab_c_p, wuv, Bp, Tp, Tp, QBP, True, 0, 0, 1)
        kr_c = cache_c_krope[l]
        past_c = jnp.concatenate([cache_c_latent[l], kr_c, kr_c, jnp.zeros((Bs, past, 2 * C_ROPE), F32)], axis=-1)
        kc_s = pad_keys(past_c, kcat[NPT:].reshape(Bs, Ts, C_KW))
        c_s = _mla_attention(q_c, kc_s, tab_c_s, wuv, Bs, Ts, Tks, QBS, False, NPT, jd_s, 0)
        c_out = jnp.concatenate([c_p.reshape(NPT, -1), c_s.reshape(NST, -1)], axis=0)

        d_p = _dsa_attention((proj16, proj32), None, None, None, tab_d_p, tri,
                             Bp, Tp, Tp, QBP, True, 0, 0, 1, topk_p)
        cd = cache_d_kv[l].reshape(Bs, past, 2, H_D * DH_D)
        kd_s = pad_keys(cd[:, :, 0], s16[:, :, COL_DK:COL_DK + 512])
        vd_s = pad_keys(cd[:, :, 1], s16[:, :, COL_DV:COL_DV + 512])
        ki_c = cache_d_kidx[l]
        ki_s = pad_keys(jnp.concatenate([ki_c, ki_c], axis=-1), s16[:, :, COL_KI:COL_KI + LANES])
        d_s = _dsa_attention((proj16, proj32), kd_s, vd_s, ki_s, tab_d_s, tri,
                             Bs, Ts, Tks, QBS, False, NPT, jd_s, 0, topk_s)
        d_out = jnp.concatenate([d_p.reshape(NPT, -1), d_s.reshape(NST, -1)], axis=0)

        outs = jnp.stack([a_out, b_out, c_out, d_out], axis=0)
        merged = _merge(h, w_gate[l], outs, w_branch[l])
        x3 = _mm_res(merged, w_out[l], x3, mod3, 2, 1024)

        h2 = _norm_mod(x3, g_ffn[l], mod3, 4, 3).reshape(M, D)
        if l % 2 == 0:
            act = _swiglu_up(h2, ffn_w1[l // 2][None], jnp.ones((M, LANES), F32), False)
            w2 = ffn_w2[l // 2]
        else:
            comb = _router(x3, g_ffn[l], mod3, 4, 3, moe_router[l // 2])
            act = _swiglu_up(h2, moe_w1[l // 2], comb, True)
            w2 = moe_w2[l // 2].reshape(-1, D)
        x3 = _mm_res(act, w2, x3, mod3, 5, 512)

        def split(arr):
            return arr[:NPT].reshape(Bp, Tp, -1), arr[NPT:].reshape(Bs, Ts, -1)
        a_kv = split(proj32[:, COL_AK:COL_AK + 1024])
        b_kv = split(proj32[:, COL_BK:COL_BK + 1024])
        d_kv = split(proj32[:, COL_DK:COL_DK + 1024])
        lat = split(lat_new)
        krn = split(kr_new[:, :C_ROPE])
        kix = split(proj32[:, COL_KI:COL_KI + IDX_DIM])
        st_p = (a_kv[0][:, Tp - min(A_WINDOW, Tp):].reshape(Bp, -1, 2, H_A, DH_A),
                b_kv[0].reshape(Bp, Tp, 2, H_B, 2 * DH_B), lat[0], krn[0],
                d_kv[0].reshape(Bp, Tp, 2, H_D, DH_D), kix[0])
        st_s = (a_kv[1].reshape(Bs, Ts, 2, H_A, DH_A), b_kv[1].reshape(Bs, Ts, 2, H_B, 2 * DH_B), lat[1], krn[1],
                d_kv[1].reshape(Bs, Ts, 2, H_D, DH_D), kix[1])
        states.append((st_p, st_s))

    y = _norm_final(x3, g_final).reshape(M, D)
    y_p = y[:NPT].reshape(Bp, Tp, D)
    y_s = y[NPT:].reshape(Bs, Ts, D)
    new_p = tuple(jnp.stack([states[l][0][i] for l in range(DEPTH)], axis=0) for i in range(6))
    new_s = tuple(jnp.stack([states[l][1][i] for l in range(DEPTH)], axis=0) for i in range(6))
    return (y_p, y_s) + new_p + new_s
```
